```python
import jax, jax.numpy as jnp
from jax import lax
import numpy as np

D_MODEL = 2048
BATCH = 4
SEQ = 4096
DEPTH = 1

CTX_LEN = 256
GRID_W = 64
EPS = 1e-6

GLA_HEADS = 4
GLA_DK = 256
GLA_DV = 512
GLA_KEY = GLA_HEADS * GLA_DK
GLA_VAL = GLA_HEADS * GLA_DV
GLA_RANK = 16
GLA_GATE_NORM = 16.0
GLA_CHUNK = 64

RNN_WIDTH = D_MODEL
RNN_BLOCKS = 16
RNN_BLOCK_W = RNN_WIDTH // RNN_BLOCKS
CONV_W = 4
CONV_PAD = (2, 1)
LRU_C = 8.0

IN_SIZES = (GLA_KEY, GLA_KEY, GLA_VAL, GLA_VAL, GLA_RANK, GLA_RANK, RNN_WIDTH, RNN_WIDTH, D_MODEL, D_MODEL)
D_IN = sum(IN_SIZES)

kernel_name = "hybrid_gla_rglru_parallel_prefix_block"


def rmsnorm(x, gain):
    xf = x.astype(jnp.float32)
    y = xf * lax.rsqrt(jnp.mean(xf * xf, axis=-1, keepdims=True) + EPS)
    return (y * gain.astype(jnp.float32)).astype(x.dtype)


def split_heads(z, dh):
    b, t, _ = z.shape
    return z.reshape(b, t, -1, dh).transpose(0, 2, 1, 3)


def gla_chunked(q, k, v, log_a, s0):
    b, h, t, _ = q.shape
    dv = v.shape[-1]
    n = t // GLA_CHUNK
    rs = lambda z: z.reshape(b, h, n, GLA_CHUNK, z.shape[-1]).astype(jnp.float32)
    q, k, v, log_a = rs(q), rs(k), rs(v), rs(log_a)
    cum = jnp.cumsum(log_a, axis=3)
    last = cum[:, :, :, -1:, :]
    q_e = q * jnp.exp(cum)
    k_e = k * jnp.exp(-cum)
    k_d = k * jnp.exp(last - cum)
    mask = jnp.tril(jnp.ones((GLA_CHUNK, GLA_CHUNK), dtype=bool))
    scores = jnp.where(mask, jnp.einsum('bhncd,bhnsd->bhncs', q_e, k_e), 0.0)
    o_intra = jnp.einsum('bhncs,bhnse->bhnce', scores, v)

    def step(state, xs):
        qc, kc, vc, dc = xs
        o = jnp.einsum('bhcd,bhde->bhce', qc, state)
        state = state * dc[..., None] + jnp.einsum('bhcd,bhce->bhde', kc, vc)
        return state, o

    xs = (jnp.moveaxis(q_e, 2, 0), jnp.moveaxis(k_d, 2, 0), jnp.moveaxis(v, 2, 0),
          jnp.moveaxis(jnp.exp(last[:, :, :, 0, :]), 2, 0))
    s_final, o_inter = lax.scan(step, s0.astype(jnp.float32), xs)
    o = o_intra + jnp.moveaxis(o_inter, 0, 2)
    return o.reshape(b, h, t, dv), s_final


def gla_bidir(q, k, v, la_f, la_b, s0_f, s0_b):
    o_f, s_f = gla_chunked(q, k, v, la_f, s0_f)
    rev = lambda z: jnp.flip(z, axis=2)
    o_b, s_b = gla_chunked(rev(q), rev(k), rev(v), rev(la_b), s0_b)
    return o_f + rev(o_b), s_f, s_b


def linear_scan(a, u, h0):
    def combine(left, right):
        a_l, u_l = left
        a_r, u_r = right
        return a_l * a_r, a_r * u_l + u_r
    a_cum, h = lax.associative_scan(combine, (a, u), axis=1)
    return h + a_cum * h0[:, None, :]


def block_diag(x, w):
    xb = x.reshape(x.shape[:-1] + (RNN_BLOCKS, RNN_BLOCK_W))
    return jnp.einsum('bthi,hij->bthj', xb, w.astype(jnp.float32)).reshape(x.shape)


def rglru_coeffs(x, w_a, b_a, w_x, b_x, lam):
    r = jax.nn.sigmoid(block_diag(x, w_a) + b_a.astype(jnp.float32))
    i = jax.nn.sigmoid(block_diag(x, w_x) + b_x.astype(jnp.float32))
    log_a = -LRU_C * jax.nn.softplus(-lam.astype(jnp.float32)) * r
    return jnp.exp(log_a), jnp.sqrt(-jnp.expm1(2.0 * log_a)) * (i * x)


def rglru_bidir(xc, w_a, b_a, w_x, b_x, lam, h0_f, h0_b):
    x32 = xc.astype(jnp.float32)
    a_f, u_f = rglru_coeffs(x32, w_a[0], b_a[0], w_x[0], b_x[0], lam[0])
    h_f = linear_scan(a_f, u_f, h0_f)
    a_b, u_b = rglru_coeffs(x32, w_a[1], b_a[1], w_x[1], b_x[1], lam[1])
    h_b = jnp.flip(linear_scan(jnp.flip(a_b, 1), jnp.flip(u_b, 1), h0_b), 1)
    return h_f + h_b, h_f[:, -1], h_b[:, 0]


def depthwise_conv(x, w, bias):
    y = lax.conv_general_dilated(x, w[:, None, :].astype(x.dtype), window_strides=(1,), padding=[CONV_PAD],
                                 dimension_numbers=('NWC', 'WIO', 'NWC'), feature_group_count=x.shape[-1])
    return y + bias.astype(x.dtype)


def project_in(h, shift, scale, g_norm, w_in):
    hn = rmsnorm(h, g_norm) * (1.0 + scale) + shift
    bounds = [int(v) for v in np.cumsum(IN_SIZES)[:-1]]
    return jnp.split(hn @ w_in, bounds, axis=-1)


def mixers(parts, s_f, s_b, h_f, h_b, w_gla_a, b_gla_a, w_conv, b_conv, w_rg_a, b_rg_a, w_rg_x, b_rg_x, lam):
    q = split_heads(parts[0], GLA_DK)
    k = split_heads(parts[1], GLA_DK) * GLA_DK ** -0.5
    v = split_heads(parts[2], GLA_DV)
    la_f = split_heads(jax.nn.log_sigmoid((parts[4] @ w_gla_a[0] + b_gla_a[0]).astype(jnp.float32)) / GLA_GATE_NORM, GLA_DK)
    la_b = split_heads(jax.nn.log_sigmoid((parts[5] @ w_gla_a[1] + b_gla_a[1]).astype(jnp.float32)) / GLA_GATE_NORM, GLA_DK)
    o_gla, s_f, s_b = gla_bidir(q, k, v, la_f, la_b, s_f, s_b)
    xc = depthwise_conv(parts[6], w_conv, b_conv)
    y_rnn, h_f, h_b = rglru_bidir(xc, w_rg_a, b_rg_a, w_rg_x, b_rg_x, lam, h_f, h_b)
    return o_gla, y_rnn, s_f, s_b, h_f, h_b


def merge_out(parts, o_gla, y_rnn, g_gla_out, w_o_gla, w_o_rnn, b_merge, w_out):
    dt = parts[3].dtype
    b, h, t, dv = o_gla.shape
    o = rmsnorm(o_gla, g_gla_out).transpose(0, 2, 1, 3).reshape(b, t, h * dv).astype(dt)
    y_gla = (o * jax.nn.silu(parts[3])) @ w_o_gla
    y_lru = (y_rnn.astype(dt) * jax.nn.silu(parts[7])) @ w_o_rnn
    gate_gla = jax.nn.sigmoid(parts[8] + b_merge[:D_MODEL])
    gate_lru = jax.nn.sigmoid(parts[9] + b_merge[D_MODEL:])
    return (gate_gla * y_gla + gate_lru * y_lru) @ w_out


def setup_inputs(seed: int = 0) -> dict:
    key = jax.random.key(seed)
    ks = jax.random.split(key, 24)
    f32 = jnp.float32
    nrm = lambda k, shape, fan_in: jax.random.normal(k, shape, f32) * fan_in ** -0.5
    x = jax.random.normal(ks[0], (BATCH, SEQ, D_MODEL), f32)
    c = jax.random.normal(ks[1], (BATCH, D_MODEL), f32)
    ctx = jax.random.normal(ks[2], (BATCH, CTX_LEN, D_MODEL), f32)
    c_ctx = jax.random.normal(ks[3], (D_MODEL,), f32)
    w_ada = nrm(ks[4], (DEPTH, D_MODEL, 3 * D_MODEL), D_MODEL) * 0.5
    b_ada = 0.02 * jax.random.normal(ks[5], (DEPTH, 3 * D_MODEL), f32)
    g_norm = 1.0 + 0.05 * jax.random.normal(ks[6], (DEPTH, D_MODEL), f32)
    w_in = nrm(ks[7], (DEPTH, D_MODEL, D_IN), D_MODEL)
    w_gla_a = nrm(ks[8], (DEPTH, 2, GLA_RANK, GLA_KEY), GLA_RANK)
    b_gla_a = 0.5 * jax.random.normal(ks[9], (DEPTH, 2, GLA_KEY), f32)
    g_gla_out = 1.0 + 0.05 * jax.random.normal(ks[10], (DEPTH, GLA_DV), f32)
    w_conv = nrm(ks[11], (DEPTH, CONV_W, RNN_WIDTH), CONV_W)
    b_conv = 0.02 * jax.random.normal(ks[12], (DEPTH, RNN_WIDTH), f32)
    w_rg_a = nrm(ks[13], (DEPTH, 2, RNN_BLOCKS, RNN_BLOCK_W, RNN_BLOCK_W), RNN_BLOCK_W)
    b_rg_a = 0.1 * jax.random.normal(ks[14], (DEPTH, 2, RNN_WIDTH), f32)
    w_rg_x = nrm(ks[15], (DEPTH, 2, RNN_BLOCKS, RNN_BLOCK_W, RNN_BLOCK_W), RNN_BLOCK_W)
    b_rg_x = 0.1 * jax.random.normal(ks[16], (DEPTH, 2, RNN_WIDTH), f32)
    u = jax.random.uniform(ks[17], (DEPTH, 2, RNN_WIDTH), f32, minval=0.9, maxval=0.999)
    a0 = u ** (1.0 / LRU_C)
    lam = jnp.log(a0) - jnp.log1p(-a0)
    w_o_gla = nrm(ks[18], (DEPTH, GLA_VAL, D_MODEL), GLA_VAL)
    w_o_rnn = nrm(ks[19], (DEPTH, RNN_WIDTH, D_MODEL), RNN_WIDTH)
    b_merge = 0.1 * jax.random.normal(ks[20], (DEPTH, 2 * D_MODEL), f32)
    w_out = nrm(ks[21], (DEPTH, D_MODEL, D_MODEL), D_MODEL)
    g_final = 1.0 + 0.05 * jax.random.normal(ks[22], (D_MODEL,), f32)
    return {"x": x, "c": c, "ctx": ctx, "c_ctx": c_ctx, "w_ada": w_ada, "b_ada": b_ada, "g_norm": g_norm,
            "w_in": w_in, "w_gla_a": w_gla_a, "b_gla_a": b_gla_a, "g_gla_out": g_gla_out, "w_conv": w_conv,
            "b_conv": b_conv, "w_rg_a": w_rg_a, "b_rg_a": b_rg_a, "w_rg_x": w_rg_x, "b_rg_x": b_rg_x, "lam": lam,
            "w_o_gla": w_o_gla, "w_o_rnn": w_o_rnn, "b_merge": b_merge, "w_out": w_out, "g_final": g_final}


def reference(x, c, ctx, c_ctx, w_ada, b_ada, g_norm, w_in, w_gla_a, b_gla_a, g_gla_out, w_conv, b_conv,
              w_rg_a, b_rg_a, w_rg_x, b_rg_x, lam, w_o_gla, w_o_rnn, b_merge, w_out, g_final):
    bsz, n_tok, _ = x.shape
    rows = n_tok // GRID_W
    assert rows * GRID_W == n_tok
    zeros_s = jnp.zeros((bsz, GLA_HEADS, GLA_DK, GLA_DV), jnp.float32)
    zeros_h = jnp.zeros((bsz, RNN_WIDTH), jnp.float32)
    for l in range(DEPTH):
        mod_x = jax.nn.silu(c) @ w_ada[l] + b_ada[l]
        mod_c = jax.nn.silu(c_ctx) @ w_ada[l] + b_ada[l]
        sh_x, sc_x, gt_x = [m[:, None, :] for m in jnp.split(mod_x, 3, axis=-1)]
        sh_c, sc_c, gt_c = jnp.split(mod_c, 3, axis=-1)
        lp = (w_gla_a[l], b_gla_a[l], w_conv[l], b_conv[l], w_rg_a[l], b_rg_a[l], w_rg_x[l], b_rg_x[l], lam[l])
        op = (g_gla_out[l], w_o_gla[l], w_o_rnn[l], b_merge[l], w_out[l])
        parts_c = project_in(ctx, sh_c, sc_c, g_norm[l], w_in[l])
        o_c, y_c, s_f, s_b, h_f, h_b = mixers(parts_c, zeros_s, zeros_s, zeros_h, zeros_h, *lp)
        parts_x = project_in(x, sh_x, sc_x, g_norm[l], w_in[l])
        o_x, y_x, _, _, _, _ = mixers(parts_x, s_f, s_b, h_f, h_b, *lp)
        x = x + gt_x * merge_out(parts_x, o_x, y_x, *op)
        if l + 1 < DEPTH:
            ctx = ctx + gt_c * merge_out(parts_c, o_c, y_c, *op)
    return rmsnorm(x, g_final)
```

```python
import functools

import jax
import jax.numpy as jnp
from jax import lax
from jax.experimental import pallas as pl
from jax.experimental.pallas import tpu as pltpu

F32 = jnp.float32
BF16 = jnp.bfloat16

EPS = 1e-6
GLA_HEADS = 4
GLA_DK = 256
GLA_DV = 512
GLA_KEY = GLA_HEADS * GLA_DK
GLA_VAL = GLA_HEADS * GLA_DV
GLA_RANK = 16
GLA_GATE_NORM = 16.0
GLA_CHUNK = 64
RNN_BLOCKS = 16
RNN_BLOCK_W = 128
CONV_W = 4
LRU_C = 8.0
LANE = 128
SUBLANE = 8
BF16_ROWS = 16
VMEM_LIMIT = 56 * 1024 * 1024


def _split_bf16(a):
    hi = a.astype(BF16)
    lo = (a - hi.astype(F32)).astype(BF16)
    return hi, lo


_NN = (((1,), (0,)), ((), ()))
_NT = (((1,), (1,)), ((), ()))
_TN = (((0,), (0,)), ((), ()))


def _dot(a, b, dims=_NN):
    return lax.dot_general(a, b, dims, preferred_element_type=F32)


def _dot3(a, b, dims=_NN):
    ah, al = _split_bf16(a)
    bh, bl = _split_bf16(b)
    return _dot(ah, bh, dims) + (_dot(ah, bl, dims) + _dot(al, bh, dims))


def _sigmoid(z):
    return 1.0 / (1.0 + jnp.exp(-z))


def _softplus(z):
    return jnp.maximum(z, 0.0) + jnp.log1p(jnp.exp(-jnp.abs(z)))


def _params(sem):
    return pltpu.CompilerParams(dimension_semantics=sem, vmem_limit_bytes=VMEM_LIMIT)


def _ada_kernel(c_ref, w_ref, b_ref, o_ref):
    c = c_ref[...]
    o_ref[...] = _dot3(c * _sigmoid(c), w_ref[...]) + b_ref[...]


def _ada(cc, w, b):
    rows, d = cc.shape
    n = w.shape[1]
    tn = 768
    return pl.pallas_call(
        _ada_kernel,
        grid=(n // tn,),
        in_specs=[pl.BlockSpec((rows, d), lambda j: (0, 0)),
                  pl.BlockSpec((d, tn), lambda j: (0, j)),
                  pl.BlockSpec((1, tn), lambda j: (0, j))],
        out_specs=pl.BlockSpec((rows, tn), lambda j: (0, j)),
        out_shape=jax.ShapeDtypeStruct((rows, n), F32),
        compiler_params=_params(("arbitrary",)),
        name="ada",
    )(cc, w, b.reshape(1, n))


def _inproj_kernel(x_ref, sh_ref, sc_ref, g_ref, w_ref, wlr_ref, out_ref, lr_ref, hn_scr, *, sub):
    tm = x_ref.shape[0]

    @pl.when(pl.program_id(1) == 0)
    def _():
        shift = sh_ref[0]
        wlr = wlr_ref[...]

        def body(s, carry):
            r0 = pl.multiple_of(s * sub, sub)
            x = x_ref[pl.ds(r0, sub), :]
            ms = jnp.mean(x * x, axis=-1, keepdims=True)
            hn = x * lax.rsqrt(ms + EPS)
            hn = (hn * g_ref[...]) * (1.0 + sc_ref[0]) + shift
            hn_scr[pl.ds(r0, sub), :] = hn.astype(BF16)
            lr_ref[pl.ds(r0, sub), :] = _dot3(hn, wlr)
            return carry

        lax.fori_loop(0, tm // sub, body, 0)

    out_ref[...] = _dot(hn_scr[...], w_ref[...]).astype(BF16)


def _inproj(h2d, mod3, mod_row, g_norm, w_main, w_lr, tm, tn):
    r, d = h2d.shape
    n = w_main.shape[1]
    sub = min(256, tm)
    return pl.pallas_call(
        functools.partial(_inproj_kernel, sub=sub),
        grid=(r // tm, n // tn),
        in_specs=[pl.BlockSpec((tm, d), lambda i, j: (i, 0)),
                  pl.BlockSpec((1, 1, d), lambda i, j: (mod_row(i), 0, 0)),
                  pl.BlockSpec((1, 1, d), lambda i, j: (mod_row(i), 0, 1)),
                  pl.BlockSpec((1, d), lambda i, j: (0, 0)),
                  pl.BlockSpec((d, tn), lambda i, j: (0, j)),
                  pl.BlockSpec((d, LANE), lambda i, j: (0, 0))],
        out_specs=[pl.BlockSpec((tm, tn), lambda i, j: (i, j)),
                   pl.BlockSpec((tm, LANE), lambda i, j: (i, 0))],
        out_shape=[jax.ShapeDtypeStruct((r, n), BF16),
                   jax.ShapeDtypeStruct((r, LANE), F32)],
        scratch_shapes=[pltpu.VMEM((tm, d), BF16)],
        compiler_params=_params(("arbitrary", "arbitrary")),
        name="inproj",
    )(h2d, mod3, mod3, g_norm.reshape(1, d), w_main, w_lr)


def _gla_kernel(lr_ref, q_ref, k_ref, v_ref, wa_ref, ba_ref, s0_ref, o_ref, sfin_ref, s_scr, *, nblk):
    d = pl.program_id(0)
    blk = pl.program_id(3)
    tb = q_ref.shape[1]
    nch = tb // GLA_CHUNK
    c = GLA_CHUNK

    @pl.when(blk == 0)
    def _():
        s_scr[...] = s0_ref[0, 0, 0]

    sgn = 1 - 2 * d
    row = lax.broadcasted_iota(jnp.int32, (c, c), 0)
    col = lax.broadcasted_iota(jnp.int32, (c, c), 1)
    keep = (row - col) * sgn >= 0
    tri = jnp.where(keep, 1.0, 0.0).astype(BF16)
    ones = jnp.ones((c, LANE), BF16)
    wa = wa_ref[0]
    ba = ba_ref[0]

    def chunk(j, carry):
        cidx = jnp.where(d == 0, j, nch - 1 - j)
        r0 = pl.multiple_of(cidx * c, c)
        lr = lr_ref[0, pl.ds(r0, c), :]
        z = _dot3(lr, wa) + ba
        la = (jnp.minimum(z, 0.0) - jnp.log1p(jnp.exp(-jnp.abs(z)))) * (1.0 / GLA_GATE_NORM)
        la_hi, la_lo = _split_bf16(la)
        cum = _dot(tri, la_hi) + _dot(tri, la_lo)
        tot = jnp.where(d == 0, cum[c - 1:c, :], cum[0:1, :])
        tot_t = _dot(la_hi, ones, _TN) + _dot(la_lo, ones, _TN)
        dec = jnp.exp(tot_t)
        dec = jnp.concatenate([dec] * (GLA_DV // LANE), axis=1)

        q = q_ref[0, pl.ds(r0, c), :].astype(F32)
        k = k_ref[0, pl.ds(r0, c), :].astype(F32) * (GLA_DK ** -0.5)
        v = v_ref[0, pl.ds(r0, c), :]
        qe = (q * jnp.exp(cum)).astype(BF16)
        ke = (k * jnp.exp(-cum)).astype(BF16)
        kd = (k * jnp.exp(tot - cum)).astype(BF16)
        sc = _dot(qe, ke, _NT)
        sc = jnp.where(keep, sc, 0.0).astype(BF16)
        s_old = s_scr[...]
        o = _dot(sc, v) + _dot(qe, s_old.astype(BF16))
        o_ref[0, 0, pl.ds(r0, c), :] = o.astype(BF16)
        s_scr[...] = s_old * dec + _dot(kd, v, _TN)
        return carry

    lax.fori_loop(0, nch, chunk, 0)

    @pl.when(blk == nblk - 1)
    def _():
        sfin_ref[0, 0, 0] = s_scr[...]


def _gla(parts, lr, wa_pad, ba, s0, tb):
    b, t, _ = parts.shape
    nblk = t // tb
    pos = lambda d, i: i + d * (nblk - 1 - 2 * i)
    kq = GLA_KEY // GLA_DK
    kv = (2 * GLA_KEY) // GLA_DV
    return pl.pallas_call(
        functools.partial(_gla_kernel, nblk=nblk),
        grid=(2, b, GLA_HEADS, nblk),
        in_specs=[pl.BlockSpec((1, tb, LANE), lambda d, bb, h, i: (bb, pos(d, i), 0)),
                  pl.BlockSpec((1, tb, GLA_DK), lambda d, bb, h, i: (bb, pos(d, i), h)),
                  pl.BlockSpec((1, tb, GLA_DK), lambda d, bb, h, i: (bb, pos(d, i), kq + h)),
                  pl.BlockSpec((1, tb, GLA_DV), lambda d, bb, h, i: (bb, pos(d, i), kv + h)),
                  pl.BlockSpec((1, LANE, GLA_DK), lambda d, bb, h, i: (d, 0, h)),
                  pl.BlockSpec((1, 1, GLA_DK), lambda d, bb, h, i: (d, 0, h)),
                  pl.BlockSpec((1, 1, 1, GLA_DK, GLA_DV), lambda d, bb, h, i: (d, bb, h, 0, 0))],
        out_specs=[pl.BlockSpec((1, 1, tb, GLA_DV), lambda d, bb, h, i: (d, bb, pos(d, i), h)),
                   pl.BlockSpec((1, 1, 1, GLA_DK, GLA_DV), lambda d, bb, h, i: (d, bb, h, 0, 0))],
        out_shape=[jax.ShapeDtypeStruct((2, b, t, GLA_VAL), BF16),
                   jax.ShapeDtypeStruct((2, b, GLA_HEADS, GLA_DK, GLA_DV), F32)],
        scratch_shapes=[pltpu.VMEM((GLA_DK, GLA_DV), F32)],
        compiler_params=_params(("arbitrary",) * 4),
        name="gla",
    )(lr, parts, parts, parts, wa_pad, ba, s0)


def _lru_kernel(xm_ref, xp_ref, xn_ref, wc_ref, bc_ref, wa_ref, ba_ref, wx_ref, bx_ref, lam_ref, h0_ref,
                h_ref, hfin_ref, xe_scr, a_scr, u_scr, hs_scr, carry_scr, *, nblk):
    d = pl.program_id(0)
    i = pl.program_id(2)
    tb = xm_ref.shape[1]
    w = xm_ref.shape[2]
    halo = BF16_ROWS
    pos = jnp.where(d == 0, i, nblk - 1 - i)

    @pl.when(i == 0)
    def _():
        carry_scr[...] = jnp.broadcast_to(h0_ref[0, 0], (SUBLANE, w))

    prev_on = jnp.where(pos > 0, 1.0, 0.0)
    next_on = jnp.where(pos < nblk - 1, 1.0, 0.0)
    xe_scr[0:halo, :] = xp_ref[0].astype(F32) * prev_on
    xe_scr[halo:halo + tb, :] = xm_ref[0].astype(F32)
    xe_scr[halo + tb:2 * halo + tb, :] = xn_ref[0].astype(F32) * next_on
    wc = wc_ref[...]
    c_lam = -LRU_C * _softplus(-lam_ref[0])

    for g in range(RNN_BLOCKS):
        sl = slice(g * RNN_BLOCK_W, (g + 1) * RNN_BLOCK_W)
        xc = bc_ref[:, sl] + wc[0:1, sl] * xe_scr[halo - 2:halo - 2 + tb, sl]
        xc = xc + wc[1:2, sl] * xe_scr[halo - 1:halo - 1 + tb, sl]
        xc = xc + wc[2:3, sl] * xe_scr[halo:halo + tb, sl]
        xc = xc + wc[3:4, sl] * xe_scr[halo + 1:halo + 1 + tb, sl]
        xcb = xc.astype(BF16)
        r = _sigmoid(_dot(xcb, wa_ref[0, g]) + ba_ref[0, :, sl])
        gi = _sigmoid(_dot(xcb, wx_ref[0, g]) + bx_ref[0, :, sl])
        log_a = c_lam[:, sl] * r
        a = jnp.exp(log_a)
        a_scr[:, sl] = a
        u_scr[:, sl] = jnp.sqrt(-jnp.tanh(log_a) * (a * a + 1.0)) * (gi * xc)

    rowi = lax.broadcasted_iota(jnp.int32, (SUBLANE, w), 0)
    nst = tb // SUBLANE

    @pl.when(d == 0)
    def _():
        def body(s, carry):
            r0 = pl.multiple_of(s * SUBLANE, SUBLANE)
            a = a_scr[pl.ds(r0, SUBLANE), :]
            u = u_scr[pl.ds(r0, SUBLANE), :]
            for sh in (1, 2, 4):
                keep = rowi >= sh
                a_s = jnp.where(keep, pltpu.roll(a, sh, 0), 1.0)
                u_s = jnp.where(keep, pltpu.roll(u, sh, 0), 0.0)
                u = u + a * u_s
                a = a * a_s
            h = u + a * carry
            hs_scr[pl.ds(r0, SUBLANE), :] = h
            return jnp.broadcast_to(h[SUBLANE - 1:SUBLANE, :], (SUBLANE, w))

        carry_scr[...] = lax.fori_loop(0, nst, body, carry_scr[...])

    @pl.when(d == 1)
    def _():
        def body(s, carry):
            r0 = pl.multiple_of((nst - 1 - s) * SUBLANE, SUBLANE)
            a = a_scr[pl.ds(r0, SUBLANE), :]
            u = u_scr[pl.ds(r0, SUBLANE), :]
            for sh in (1, 2, 4):
                keep = rowi < SUBLANE - sh
                a_s = jnp.where(keep, pltpu.roll(a, SUBLANE - sh, 0), 1.0)
                u_s = jnp.where(keep, pltpu.roll(u, SUBLANE - sh, 0), 0.0)
                u = u + a * u_s
                a = a * a_s
            h = u + a * carry
            hs_scr[pl.ds(r0, SUBLANE), :] = h
            return jnp.broadcast_to(h[0:1, :], (SUBLANE, w))

        carry_scr[...] = lax.fori_loop(0, nst, body, carry_scr[...])

    h_ref[0, 0] = hs_scr[...].astype(BF16)

    @pl.when(i == nblk - 1)
    def _():
        hfin_ref[0, 0] = carry_scr[0:1, :]


def _lru(parts, col_blk, w_conv, b_conv, wa, ba, wx, bx, lam, h0, tb):
    b, t, _ = parts.shape
    w = w_conv.shape[1]
    nblk = t // tb
    hb = tb // BF16_ROWS
    nhalo = t // BF16_ROWS
    pos = lambda d, i: i + d * (nblk - 1 - 2 * i)
    return pl.pallas_call(
        functools.partial(_lru_kernel, nblk=nblk),
        grid=(2, b, nblk),
        in_specs=[pl.BlockSpec((1, tb, w), lambda d, bb, i: (bb, pos(d, i), col_blk)),
                  pl.BlockSpec((1, BF16_ROWS, w),
                               lambda d, bb, i: (bb, jnp.maximum(pos(d, i) * hb - 1, 0), col_blk)),
                  pl.BlockSpec((1, BF16_ROWS, w),
                               lambda d, bb, i: (bb, jnp.minimum((pos(d, i) + 1) * hb, nhalo - 1), col_blk)),
                  pl.BlockSpec((CONV_W, w), lambda d, bb, i: (0, 0)),
                  pl.BlockSpec((1, w), lambda d, bb, i: (0, 0)),
                  pl.BlockSpec((1, RNN_BLOCKS, RNN_BLOCK_W, RNN_BLOCK_W), lambda d, bb, i: (d, 0, 0, 0)),
                  pl.BlockSpec((1, 1, w), lambda d, bb, i: (d, 0, 0)),
                  pl.BlockSpec((1, RNN_BLOCKS, RNN_BLOCK_W, RNN_BLOCK_W), lambda d, bb, i: (d, 0, 0, 0)),
                  pl.BlockSpec((1, 1, w), lambda d, bb, i: (d, 0, 0)),
                  pl.BlockSpec((1, 1, w), lambda d, bb, i: (d, 0, 0)),
                  pl.BlockSpec((1, 1, 1, w), lambda d, bb, i: (d, bb, 0, 0))],
        out_specs=[pl.BlockSpec((1, 1, tb, w), lambda d, bb, i: (d, bb, pos(d, i), 0)),
                   pl.BlockSpec((1, 1, 1, w), lambda d, bb, i: (d, bb, 0, 0))],
        out_shape=[jax.ShapeDtypeStruct((2, b, t, w), BF16),
                   jax.ShapeDtypeStruct((2, b, 1, w), F32)],
        scratch_shapes=[pltpu.VMEM((tb + 2 * BF16_ROWS, w), F32),
                        pltpu.VMEM((tb, w), F32),
                        pltpu.VMEM((tb, w), F32),
                        pltpu.VMEM((tb, w), F32),
                        pltpu.VMEM((SUBLANE, w), F32)],
        compiler_params=_params(("arbitrary",) * 3),
        name="lru",
    )(parts, parts, parts, w_conv, b_conv.reshape(1, w), wa, ba, wx, bx, lam, h0)


def _merge_kernel(of_ref, ob_ref, hf_ref, hb_ref, pg_ref, pl_ref, mg_ref, ml_ref, x_ref, gt_ref,
                  gh_ref, bm_ref, wog_ref, wor_ref, wout_ref, gf_ref, out_ref):
    d = x_ref.shape[2]
    o = of_ref[0, 0].astype(F32) + ob_ref[0, 0].astype(F32)
    gh = gh_ref[...]
    heads = []
    for h in range(GLA_HEADS):
        seg = o[:, h * GLA_DV:(h + 1) * GLA_DV]
        ms = jnp.mean(seg * seg, axis=-1, keepdims=True)
        heads.append(seg * lax.rsqrt(ms + EPS) * gh)
    on = jnp.concatenate(heads, axis=1)
    pg = pg_ref[0].astype(F32)
    y_gla = _dot((on * (pg * _sigmoid(pg))).astype(BF16), wog_ref[...])
    y = hf_ref[0, 0].astype(F32) + hb_ref[0, 0].astype(F32)
    pr = pl_ref[0].astype(F32)
    y_lru = _dot((y * (pr * _sigmoid(pr))).astype(BF16), wor_ref[...])
    bm = bm_ref[...]
    gate_gla = _sigmoid(mg_ref[0].astype(F32) + bm[:, :d])
    gate_lru = _sigmoid(ml_ref[0].astype(F32) + bm[:, d:])
    m = _dot((gate_gla * y_gla + gate_lru * y_lru).astype(BF16), wout_ref[...])
    xo = x_ref[0] + gt_ref[0] * m
    ms = jnp.mean(xo * xo, axis=-1, keepdims=True)
    out_ref[0] = xo * lax.rsqrt(ms + EPS) * gf_ref[...]


def _merge(o_gla, h_lru, parts, cols, x, mod3, g_gla_out, b_merge, w_o_gla, w_o_rnn, w_out, g_final, tm):
    b, t, d = x.shape
    c_gate, c_lgate, c_mg, c_ml = cols
    tok = lambda j: pl.BlockSpec((1, tm, d), lambda bb, i, j=j: (bb, i, j))
    dirb = lambda dd: pl.BlockSpec((1, 1, tm, d), lambda bb, i, dd=dd: (dd, bb, i, 0))
    const = lambda shape: pl.BlockSpec(shape, lambda bb, i: (0,) * len(shape), pipeline_mode=pl.Buffered(1))
    return pl.pallas_call(
        _merge_kernel,
        grid=(b, t // tm),
        in_specs=[dirb(0), dirb(1), dirb(0), dirb(1),
                  tok(c_gate), tok(c_lgate), tok(c_mg), tok(c_ml),
                  pl.BlockSpec((1, tm, d), lambda bb, i: (bb, i, 0)),
                  pl.BlockSpec((1, 1, d), lambda bb, i: (bb, 0, 2)),
                  const((1, GLA_DV)), const((1, 2 * d)),
                  const((GLA_VAL, d)), const((d, d)), const((d, d)), const((1, d))],
        out_specs=pl.BlockSpec((1, tm, d), lambda bb, i: (bb, i, 0)),
        out_shape=jax.ShapeDtypeStruct((b, t, d), F32),
        compiler_params=_params(("arbitrary", "arbitrary")),
        name="merge",
    )(o_gla, o_gla, h_lru, h_lru, parts, parts, parts, parts, x, mod3,
      g_gla_out.reshape(1, GLA_DV), b_merge.reshape(1, 2 * d), w_o_gla, w_o_rnn, w_out, g_final.reshape(1, d))


def kernel(x, c, ctx, c_ctx, w_ada, b_ada, g_norm, w_in, w_gla_a, b_gla_a, g_gla_out, w_conv, b_conv,
           w_rg_a, b_rg_a, w_rg_x, b_rg_x, lam, w_o_gla, w_o_rnn, b_merge, w_out, g_final):
    assert w_ada.shape[0] == 1, "single-layer problem"
    bsz, t, d = x.shape
    tc = ctx.shape[1]
    l = 0

    o_q, o_k, o_v, o_g = 0, GLA_KEY, 2 * GLA_KEY, 2 * GLA_KEY + GLA_VAL
    o_lr = o_g + GLA_VAL
    o_rest = o_lr + 2 * GLA_RANK
    wi = w_in[l]
    w_main = jnp.concatenate([wi[:, :o_lr], wi[:, o_rest:]], axis=1).astype(BF16)
    w_lr = jnp.pad(wi[:, o_lr:o_rest], ((0, 0), (0, LANE - 2 * GLA_RANK)))
    wa_pad = jnp.zeros((2, LANE, GLA_KEY), F32)
    wa_pad = wa_pad.at[0, :GLA_RANK].set(w_gla_a[l, 0]).at[1, GLA_RANK:2 * GLA_RANK].set(w_gla_a[l, 1])
    ba = b_gla_a[l].reshape(2, 1, GLA_KEY)
    c_gate, c_lin, c_lgate, c_mg, c_ml = 2, 3, 4, 5, 6

    cc = jnp.concatenate([c, c_ctx[None, :], jnp.zeros((SUBLANE - bsz - 1, d), F32)], axis=0)
    mod3 = _ada(cc, w_ada[l], b_ada[l]).reshape(SUBLANE, 1, 3 * d)

    tm_x = min(1024, t)
    tiles_per_batch = t // tm_x
    parts_c, lr_c = _inproj(ctx.reshape(bsz * tc, d), mod3, lambda i: bsz, g_norm[l], w_main, w_lr,
                            tm=bsz * tc, tn=1024)
    parts_x, lr_x = _inproj(x.reshape(bsz * t, d), mod3, lambda i: i // tiles_per_batch, g_norm[l], w_main,
                            w_lr, tm=tm_x, tn=1024)
    parts_c = parts_c.reshape(bsz, tc, -1)
    parts_x = parts_x.reshape(bsz, t, -1)
    lr_c = lr_c.reshape(bsz, tc, LANE)
    lr_x = lr_x.reshape(bsz, t, LANE)

    s0 = jnp.zeros((2, bsz, GLA_HEADS, GLA_DK, GLA_DV), F32)
    _, s_ctx = _gla(parts_c, lr_c, wa_pad, ba, s0, tb=tc)
    o_gla, _ = _gla(parts_x, lr_x, wa_pad, ba, s_ctx, tb=min(512, t))

    h0 = jnp.zeros((2, bsz, 1, d), F32)
    lru_w = (w_conv[l], b_conv[l], w_rg_a[l].astype(BF16), b_rg_a[l].reshape(2, 1, d),
             w_rg_x[l].astype(BF16), b_rg_x[l].reshape(2, 1, d), lam[l].reshape(2, 1, d))
    _, h_ctx = _lru(parts_c, c_lin, *lru_w, h0, tb=tc)
    h_lru, _ = _lru(parts_x, c_lin, *lru_w, h_ctx, tb=min(256, t))

    return _merge(o_gla, h_lru, parts_x, (c_gate, c_lgate, c_mg, c_ml), x, mod3, g_gla_out[l], b_merge[l],
                  w_o_gla[l].astype(BF16), w_o_rnn[l].astype(BF16), w_out[l].astype(BF16), g_final, tm=min(256, t))
```
